```python
import math
import jax, jax.numpy as jnp
from jax import lax
import numpy as np

D_MODEL = 1024
BATCH = 8
SEQ = 4096
DEPTH = 1

PLE_DIM = 256
SB_HEADS = 8
SB_HEAD_DIM = 64
SB_WIDTH = SB_HEADS * SB_HEAD_DIM
SB_BLOCK_Q = 128
SB_SCALE = 1.0 / math.sqrt(SB_HEAD_DIM)
ML_HEADS = 4
ML_HEAD_DIM = 128
ML_WIDTH = ML_HEADS * ML_HEAD_DIM
ML_CHUNK = 64
CONV_WIDTH = 4
D_FF = 4 * D_MODEL
RMS_EPS = 1e-6
IN_SIZES = (SB_WIDTH, SB_WIDTH, SB_WIDTH, ML_WIDTH, ML_WIDTH, ML_WIDTH, ML_WIDTH, ML_HEADS, ML_HEADS, D_MODEL, D_MODEL)
IN_COLS = sum(IN_SIZES)
IN_OFFSETS = tuple(int(v) for v in np.cumsum(IN_SIZES)[:-1])

kernel_name = "hybrid_stickbreak_mlstm_gated_block"


def rmsnorm(x, g):
    xf = x.astype(jnp.float32)
    y = xf * lax.rsqrt(jnp.mean(xf * xf, axis=-1, keepdims=True) + RMS_EPS)
    return (y * g.astype(jnp.float32)).astype(x.dtype)


def causal_depthwise_conv(u, w, b):
    c = u.shape[-1]
    y = lax.conv_general_dilated(u, w[:, None, :].astype(u.dtype), window_strides=(1,),
                                 padding=[(CONV_WIDTH - 1, 0)],
                                 dimension_numbers=('NWC', 'WIO', 'NWC'),
                                 feature_group_count=c)
    return y + b.astype(y.dtype)


def split_heads(u, n_heads):
    bsz, seq, _ = u.shape
    return u.reshape(bsz, seq, n_heads, -1).transpose(0, 2, 1, 3)


def merge_heads(u):
    bsz, nh, seq, dh = u.shape
    return u.transpose(0, 2, 1, 3).reshape(bsz, seq, nh * dh)


def stick_breaking_attention(q, k, v):
    seq = q.shape[2]
    outs = []
    for blk in range(seq // SB_BLOCK_Q):
        t0 = blk * SB_BLOCK_Q
        kv_len = t0 + SB_BLOCK_Q
        z = jnp.einsum('bhqd,bhkd->bhqk', q[:, :, t0:kv_len], k[:, :, :kv_len]) * SB_SCALE
        mask = jnp.arange(kv_len)[None, :] < (t0 + jnp.arange(SB_BLOCK_Q))[:, None]
        log_beta = jax.nn.log_sigmoid(z)
        log_keep = jnp.where(mask, log_beta - z, 0.0)
        log_after = lax.cumsum(log_keep, axis=3, reverse=True) - log_keep
        a = jnp.where(mask, jnp.exp(log_beta + log_after), 0.0)
        outs.append(jnp.einsum('bhqk,bhkd->bhqd', a, v[:, :, :kv_len]))
    return jnp.concatenate(outs, axis=2)


def mlstm_chunkwise(q, k, v, i_pre, log_f):
    bsz, nh, seq, dh = q.shape
    nc = seq // ML_CHUNK
    k = k * (dh ** -0.5)

    def to_chunks(a):
        a = a.reshape(bsz, nh, nc, ML_CHUNK, *a.shape[3:])
        return jnp.moveaxis(a, 2, 0)

    causal = jnp.tril(jnp.ones((ML_CHUNK, ML_CHUNK), dtype=bool))

    def step(carry, inp):
        c_prev, n_prev, m_prev = carry
        qc, kc, vc, ic, fc = inp
        b = jnp.cumsum(fc, axis=-1)
        d = b[..., :, None] - b[..., None, :] + ic[..., None, :]
        d = jnp.where(causal, d, -jnp.inf)
        m_inter = b + m_prev[..., None]
        m_t = jnp.maximum(m_inter, jnp.max(d, axis=-1))
        s_mat = jnp.einsum('bhtd,bhsd->bhts', qc, kc) * jnp.exp(d - m_t[..., None])
        inter = jnp.exp(m_inter - m_t)
        num = jnp.einsum('bhts,bhsd->bhtd', s_mat, vc) + inter[..., None] * jnp.einsum('bhvk,bhtk->bhtv', c_prev, qc)
        den = jnp.sum(s_mat, axis=-1) + inter * jnp.einsum('bhk,bhtk->bht', n_prev, qc)
        h = num / jnp.maximum(jnp.abs(den), jnp.exp(-m_t))[..., None]
        b_end = b[..., -1]
        g = b_end[..., None] - b + ic
        m_new = jnp.maximum(b_end + m_prev, jnp.max(g, axis=-1))
        w_s = jnp.exp(g - m_new[..., None])
        decay = jnp.exp(b_end + m_prev - m_new)
        c_new = decay[..., None, None] * c_prev + jnp.einsum('bhs,bhsv,bhsk->bhvk', w_s, vc, kc)
        n_new = decay[..., None] * n_prev + jnp.einsum('bhs,bhsk->bhk', w_s, kc)
        return (c_new, n_new, m_new), h

    init = (jnp.zeros((bsz, nh, dh, dh), jnp.float32),
            jnp.zeros((bsz, nh, dh), jnp.float32),
            jnp.zeros((bsz, nh), jnp.float32))
    xs = (to_chunks(q), to_chunks(k), to_chunks(v), to_chunks(i_pre), to_chunks(log_f))
    _, h = lax.scan(step, init, xs)
    return jnp.moveaxis(h, 0, 2).reshape(bsz, nh, seq, dh)


def setup_inputs(seed: int = 0) -> dict:
    key = jax.random.key(seed)
    ks = jax.random.split(key, 24)

    def nrm(k, shape, scale):
        return jax.random.normal(k, shape, jnp.float32) * scale

    def gain(k, n):
        return 1.0 + 0.05 * jax.random.normal(k, (DEPTH, n), jnp.float32)

    return {
        "x": nrm(ks[0], (BATCH, SEQ, D_MODEL), 1.0),
        "p": nrm(ks[1], (DEPTH, BATCH, SEQ, PLE_DIM), 1.0),
        "g_mix_pre": gain(ks[2], D_MODEL),
        "w_in": nrm(ks[3], (DEPTH, D_MODEL, IN_COLS), D_MODEL ** -0.5),
        "b_igate": nrm(ks[4], (DEPTH, ML_HEADS), 0.1),
        "b_fgate": 3.0 + nrm(ks[5], (DEPTH, ML_HEADS), 0.5),
        "w_conv": nrm(ks[6], (DEPTH, CONV_WIDTH, 2 * ML_WIDTH), CONV_WIDTH ** -0.5),
        "b_conv": nrm(ks[7], (DEPTH, 2 * ML_WIDTH), 0.02),
        "g_mlstm_head": gain(ks[8], ML_WIDTH),
        "w_branch_sb": nrm(ks[9], (DEPTH, SB_WIDTH, D_MODEL), SB_WIDTH ** -0.5),
        "w_branch_ml": nrm(ks[10], (DEPTH, ML_WIDTH, D_MODEL), ML_WIDTH ** -0.5),
        "w_out": nrm(ks[11], (DEPTH, D_MODEL, D_MODEL), D_MODEL ** -0.5),
        "g_mix_post": gain(ks[12], D_MODEL),
        "g_mlp_pre": gain(ks[13], D_MODEL),
        "w_mlp_up": nrm(ks[14], (DEPTH, D_MODEL, D_FF), D_MODEL ** -0.5),
        "w_mlp_down": nrm(ks[15], (DEPTH, D_FF, D_MODEL), D_FF ** -0.5),
        "g_mlp_post": gain(ks[16], D_MODEL),
        "g_ple_pre": gain(ks[17], D_MODEL),
        "w_ple_gate": nrm(ks[18], (DEPTH, D_MODEL, D_MODEL), D_MODEL ** -0.5),
        "w_ple_proj": nrm(ks[19], (DEPTH, PLE_DIM, D_MODEL), PLE_DIM ** -0.5),
        "g_ple_post": gain(ks[20], D_MODEL),
    }


def reference(x, p, g_mix_pre, w_in, b_igate, b_fgate, w_conv, b_conv, g_mlstm_head,
              w_branch_sb, w_branch_ml, w_out, g_mix_post, g_mlp_pre, w_mlp_up, w_mlp_down,
              g_mlp_post, g_ple_pre, w_ple_gate, w_ple_proj, g_ple_post):
    f32 = jnp.float32
    bsz, seq, _ = x.shape
    for layer in range(DEPTH):
        h = rmsnorm(x, g_mix_pre[layer])
        z = h @ w_in[layer]
        (q_sb, k_sb, v_sb, q_ml, k_ml, v_ml, o_ml, i_ml, f_ml,
         gate_sb, gate_ml) = jnp.split(z, IN_OFFSETS, axis=-1)

        y_sb = stick_breaking_attention(split_heads(q_sb.astype(f32), SB_HEADS),
                                        split_heads(k_sb.astype(f32), SB_HEADS),
                                        split_heads(v_sb.astype(f32), SB_HEADS))
        y_sb = merge_heads(y_sb).astype(x.dtype)

        qk = jax.nn.silu(causal_depthwise_conv(jnp.concatenate([q_ml, k_ml], axis=-1),
                                               w_conv[layer], b_conv[layer]))
        q_c, k_c = jnp.split(qk, 2, axis=-1)
        i_pre = (i_ml + b_igate[layer]).astype(f32).transpose(0, 2, 1)
        log_f = jax.nn.log_sigmoid((f_ml + b_fgate[layer]).astype(f32)).transpose(0, 2, 1)
        h_ml = mlstm_chunkwise(split_heads(q_c.astype(f32), ML_HEADS),
                               split_heads(k_c.astype(f32), ML_HEADS),
                               split_heads(v_ml.astype(f32), ML_HEADS), i_pre, log_f)
        h_ml = h_ml.transpose(0, 2, 1, 3)
        h_ml = h_ml * lax.rsqrt(jnp.mean(h_ml * h_ml, axis=-1, keepdims=True) + RMS_EPS)
        h_ml = h_ml.reshape(bsz, seq, ML_WIDTH) * g_mlstm_head[layer].astype(f32)
        y_ml = (h_ml * jax.nn.sigmoid(o_ml.astype(f32))).astype(x.dtype)

        merged = (jax.nn.sigmoid(gate_sb) * (y_sb @ w_branch_sb[layer])
                  + jax.nn.sigmoid(gate_ml) * (y_ml @ w_branch_ml[layer]))
        x = x + rmsnorm(merged @ w_out[layer], g_mix_post[layer])

        u = jax.nn.relu(rmsnorm(x, g_mlp_pre[layer]) @ w_mlp_up[layer])
        x = x + rmsnorm((u * u) @ w_mlp_down[layer], g_mlp_post[layer])

        gate = jax.nn.sigmoid(rmsnorm(x, g_ple_pre[layer]) @ w_ple_gate[layer])
        x = x + rmsnorm(gate * (p[layer] @ w_ple_proj[layer]), g_ple_post[layer])
    return x
```

```python
import math
from functools import partial

import jax
import jax.numpy as jnp
from jax import lax
from jax.experimental import pallas as pl
from jax.experimental.pallas import tpu as pltpu

D_MODEL = 1024
PLE_DIM = 256
SB_HEADS = 8
SB_HEAD_DIM = 64
SB_WIDTH = SB_HEADS * SB_HEAD_DIM
SB_SCALE = 1.0 / math.sqrt(SB_HEAD_DIM)
ML_HEADS = 4
ML_HEAD_DIM = 128
ML_WIDTH = ML_HEADS * ML_HEAD_DIM
CONV_WIDTH = 4
D_FF = 4 * D_MODEL
RMS_EPS = 1e-6

LANES = 128
SUBLANES = 8
VMEM_LIMIT = 52 * 1024 * 1024

COL_SB = 0
COL_U = 3 * SB_WIDTH
COL_V = COL_U + 2 * ML_WIDTH
COL_O = COL_V + ML_WIDTH
COL_GATE = COL_O + ML_WIDTH
COL_IF = COL_GATE + 2 * D_MODEL
IN_COLS_PADDED = COL_IF + LANES

TOKEN_TILE = 256
SB_TILE = 256
ML_CHUNK = 256
MASKED = -1e30

f32 = jnp.float32
bf16 = jnp.bfloat16


def _rmsnorm(x, g):
    return x * lax.rsqrt(jnp.mean(x * x, axis=-1, keepdims=True) + RMS_EPS) * g


def _mm(a, b):
    return jnp.dot(a, b, preferred_element_type=f32)


def _mm_nt(a, b):
    return lax.dot_general(a, b, (((1,), (1,)), ((), ())), preferred_element_type=f32)


def _resident(shape):
    return pl.BlockSpec(shape, lambda *_: (0,) * len(shape), pipeline_mode=pl.Buffered(1))


def _in_proj_kernel(x_ref, g_ref, w_ref, sb_ref, u_ref, v_ref, o_ref, gate_ref, if_ref):
    h = _rmsnorm(x_ref[...], g_ref[...]).astype(bf16)

    def proj(lo, hi):
        return _mm(h, w_ref[:, lo:hi])

    sb_ref[:, 0:SB_WIDTH] = (proj(COL_SB, COL_SB + SB_WIDTH) * SB_SCALE).astype(bf16)
    sb_ref[:, SB_WIDTH:] = proj(COL_SB + SB_WIDTH, COL_U).astype(bf16)
    u_ref[...] = proj(COL_U, COL_V)
    v_ref[...] = proj(COL_V, COL_O).astype(bf16)
    o_ref[...] = proj(COL_O, COL_GATE)
    gate_ref[...] = proj(COL_GATE, COL_IF)
    if_ref[...] = proj(COL_IF, IN_COLS_PADDED)


def _in_proj(x2d, g, w):
    n = x2d.shape[0]
    tm = TOKEN_TILE
    row = lambda width: pl.BlockSpec((tm, width), lambda i: (i, 0))
    return pl.pallas_call(
        _in_proj_kernel,
        grid=(n // tm,),
        in_specs=[row(D_MODEL), _resident((1, D_MODEL)), _resident((D_MODEL, IN_COLS_PADDED))],
        out_specs=[row(3 * SB_WIDTH), row(2 * ML_WIDTH), row(ML_WIDTH), row(ML_WIDTH),
                   row(2 * D_MODEL), row(LANES)],
        out_shape=[jax.ShapeDtypeStruct((n, 3 * SB_WIDTH), bf16),
                   jax.ShapeDtypeStruct((n, 2 * ML_WIDTH), f32),
                   jax.ShapeDtypeStruct((n, ML_WIDTH), bf16),
                   jax.ShapeDtypeStruct((n, ML_WIDTH), f32),
                   jax.ShapeDtypeStruct((n, 2 * D_MODEL), f32),
                   jax.ShapeDtypeStruct((n, LANES), f32)],
        compiler_params=pltpu.CompilerParams(dimension_semantics=("arbitrary",),
                                             vmem_limit_bytes=VMEM_LIMIT),
        name="in_proj",
    )(x2d, g, w)


def _sb_attn_kernel(q_ref, k_ref, v_ref, o_ref, acc_ref):
    t = SB_TILE
    seq = q_ref.shape[0]
    row = lax.broadcasted_iota(jnp.int32, (t, t), 0)
    col = lax.broadcasted_iota(jnp.int32, (t, t), 1)
    later = (row > col).astype(bf16)
    causal = col < row
    lane = lax.broadcasted_iota(jnp.int32, (t, LANES), 1)
    first_head = lane < SB_HEAD_DIM

    def block(qh, kj, carry, masked):
        start = pl.multiple_of(kj * t, t)
        k2 = k_ref[pl.ds(start, t), :]
        v2 = v_ref[pl.ds(start, t), :]
        z = _mm_nt(qh, k2)
        log_beta = jax.nn.log_sigmoid(z)
        log_keep = log_beta - z
        if masked:
            log_keep = jnp.where(causal, log_keep, 0.0)
        log_after = _mm(log_keep.astype(bf16), later) + carry
        a = jnp.exp(log_beta + log_after)
        if masked:
            a = jnp.where(causal, a, 0.0)
        pv = _mm(a.astype(bf16), v2)
        return pv, carry + jnp.sum(log_keep, axis=1, keepdims=True)

    def q_tile(qi, _):
        q2 = q_ref[pl.ds(pl.multiple_of(qi * t, t), t), :]
        qa = jnp.where(first_head, q2, jnp.zeros_like(q2))
        qb = jnp.where(first_head, jnp.zeros_like(q2), q2)
        zero = jnp.zeros((t, 1), f32)
        pva, ca = block(qa, qi, zero, True)
        pvb, cb = block(qb, qi, zero, True)
        acc_ref[0] = pva
        acc_ref[1] = pvb

        def kv_step(i, carry):
            ca, cb = carry
            kj = qi - 1 - i
            pva, ca = block(qa, kj, ca, False)
            pvb, cb = block(qb, kj, cb, False)
            acc_ref[0] += pva
            acc_ref[1] += pvb
            return ca, cb

        lax.fori_loop(0, qi, kv_step, (ca, cb))
        out = jnp.where(first_head, acc_ref[0], acc_ref[1])
        o_ref[pl.ds(pl.multiple_of(qi * t, t), t), :] = out.astype(o_ref.dtype)
        return 0

    lax.fori_loop(0, seq // t, q_tile, 0)


def _sb_attn(qkv):
    bsz, seq, _ = qkv.shape
    pairs = SB_WIDTH // LANES
    spec = lambda off: pl.BlockSpec((None, seq, LANES), lambda b, hp: (b, 0, off + hp))
    return pl.pallas_call(
        _sb_attn_kernel,
        grid=(bsz, pairs),
        in_specs=[spec(0), spec(pairs), spec(2 * pairs)],
        out_specs=spec(0),
        out_shape=jax.ShapeDtypeStruct((bsz, seq, SB_WIDTH), bf16),
        scratch_shapes=[pltpu.VMEM((2, SB_TILE, LANES), f32)],
        compiler_params=pltpu.CompilerParams(dimension_semantics=("arbitrary", "arbitrary"),
                                             vmem_limit_bytes=VMEM_LIMIT),
        name="sb_attn",
    )(qkv, qkv, qkv)


def _split3(x):
    hi = x.astype(bf16)
    r1 = x - hi.astype(f32)
    mid = r1.astype(bf16)
    lo = (r1 - mid.astype(f32)).astype(bf16)
    return hi, mid, lo


def _mlstm_kernel(u_ref, v_ref, o_ref, if_ref, wconv_ref, bconv_ref, bias_ref, ghead_ref,
                  y_ref, state_ref, m_ref, tail_ref):
    n = ML_CHUNK
    dh = ML_HEAD_DIM

    @pl.when(pl.program_id(1) == 0)
    def _():
        state_ref[...] = jnp.zeros_like(state_ref)
        m_ref[...] = jnp.zeros_like(m_ref)
        tail_ref[...] = jnp.zeros_like(tail_ref)

    u = u_ref[...]
    tail = tail_ref[...]
    row8 = lax.broadcasted_iota(jnp.int32, tail.shape, 0)
    acc = u * wconv_ref[CONV_WIDTH - 1:CONV_WIDTH, :] + bconv_ref[...]
    for back in range(1, CONV_WIDTH):
        rolled = pltpu.roll(u, back, 0)
        top = jnp.where(row8 < back, pltpu.roll(tail, back, 0), rolled[0:SUBLANES])
        shifted = jnp.concatenate([top, rolled[SUBLANES:]], axis=0)
        acc = acc + shifted * wconv_ref[CONV_WIDTH - 1 - back:CONV_WIDTH - back, :]
    tail_ref[...] = u[n - SUBLANES:n]
    qk = acc * jax.nn.sigmoid(acc)

    row = lax.broadcasted_iota(jnp.int32, (n, n), 0)
    col = lax.broadcasted_iota(jnp.int32, (n, n), 1)
    not_after = row >= col
    pre = if_ref[...] + bias_ref[...]
    pre_t = pre.T[0:2 * SUBLANES]
    log_f = jax.nn.log_sigmoid(pre)
    log_f_t = jax.nn.log_sigmoid(pre_t)
    lower = not_after.astype(bf16)
    upper = (row <= col).astype(bf16)
    b_cols = sum(_mm(lower, part) for part in _split3(log_f))
    b_rows = sum(_mm(part, upper) for part in _split3(log_f_t))

    ones_col = (lax.broadcasted_iota(jnp.int32, (n, dh), 1) == 0).astype(bf16)

    for h in range(ML_HEADS):
        q = qk[:, h * dh:(h + 1) * dh].astype(bf16)
        k = qk[:, ML_WIDTH + h * dh:ML_WIDTH + (h + 1) * dh] * (dh ** -0.5)
        v_ext = jnp.concatenate([v_ref[:, h * dh:(h + 1) * dh], ones_col], axis=1)
        b_col = b_cols[:, ML_HEADS + h:ML_HEADS + h + 1]
        i_col = pre[:, h:h + 1]
        b_row = b_rows[ML_HEADS + h:ML_HEADS + h + 1, :]
        i_row = pre_t[h:h + 1, :]
        m_prev = m_ref[h][0:1, 0:1]
        state = state_ref[h]

        d = jnp.where(not_after, b_col - (b_row - i_row), MASKED)
        m_inter = b_col + m_prev
        m_t = jnp.maximum(m_inter, jnp.max(d, axis=1, keepdims=True))
        s_mat = _mm_nt(q, k.astype(bf16)) * jnp.exp(d - m_t)
        inter = jnp.exp(m_inter - m_t)
        numden = _mm(s_mat.astype(bf16), v_ext) + inter * _mm(q, state.astype(bf16))
        den = numden[:, dh:dh + 1]
        hid = numden[:, 0:dh] * (1.0 / jnp.maximum(jnp.abs(den), jnp.exp(-m_t)))

        hid = hid * lax.rsqrt(jnp.mean(hid * hid, axis=-1, keepdims=True) + RMS_EPS)
        hid = hid * ghead_ref[:, h * dh:(h + 1) * dh]
        y = hid * jax.nn.sigmoid(o_ref[:, h * dh:(h + 1) * dh])
        y_ref[:, h * dh:(h + 1) * dh] = y.astype(y_ref.dtype)

        b_end = b_col[n - 1:n, :]
        g = b_end - b_col + i_col
        m_new = jnp.maximum(b_end + m_prev, jnp.max(g, axis=0, keepdims=True))
        w_s = jnp.exp(g - m_new)
        decay = jnp.exp(b_end + m_prev - m_new)
        wv = (w_s * v_ext.astype(f32)).astype(bf16)
        state_ref[h] = decay * state + _mm(k.T.astype(bf16), wv)
        m_ref[h] = jnp.broadcast_to(m_new, (SUBLANES, LANES))


def _mlstm(u, v, o, i_f, w_conv, b_conv, gate_bias, g_head):
    bsz, seq, _ = u.shape
    n = ML_CHUNK
    tok = lambda width: pl.BlockSpec((None, n, width), lambda b, c: (b, c, 0))
    return pl.pallas_call(
        _mlstm_kernel,
        grid=(bsz, seq // n),
        in_specs=[tok(2 * ML_WIDTH), tok(ML_WIDTH), tok(ML_WIDTH), tok(LANES),
                  _resident((CONV_WIDTH, 2 * ML_WIDTH)), _resident((1, 2 * ML_WIDTH)),
                  _resident((1, LANES)), _resident((1, ML_WIDTH))],
        out_specs=tok(ML_WIDTH),
        out_shape=jax.ShapeDtypeStruct((bsz, seq, ML_WIDTH), bf16),
        scratch_shapes=[pltpu.VMEM((ML_HEADS, ML_HEAD_DIM, 2 * ML_HEAD_DIM), f32),
                        pltpu.VMEM((ML_HEADS, SUBLANES, LANES), f32),
                        pltpu.VMEM((SUBLANES, 2 * ML_WIDTH), f32)],
        compiler_params=pltpu.CompilerParams(dimension_semantics=("arbitrary", "arbitrary"),
                                             vmem_limit_bytes=VMEM_LIMIT),
        name="mlstm",
    )(u, v, o, i_f, w_conv, b_conv, gate_bias, g_head)


def _merge_out_kernel(x_ref, ysb_ref, yml_ref, gate_ref, wsb_ref, wml_ref, wout_ref, g_ref,
                      out_ref):
    gate_sb = jax.nn.sigmoid(gate_ref[:, 0:D_MODEL])
    gate_ml = jax.nn.sigmoid(gate_ref[:, D_MODEL:])
    merged = gate_sb * _mm(ysb_ref[...], wsb_ref[...]) + gate_ml * _mm(yml_ref[...], wml_ref[...])
    mixed = _mm(merged.astype(bf16), wout_ref[...])
    out_ref[...] = x_ref[...] + _rmsnorm(mixed, g_ref[...])


def _merge_out(x2d, y_sb, y_ml, gates, w_sb, w_ml, w_out, g):
    n = x2d.shape[0]
    tm = TOKEN_TILE
    row = lambda width: pl.BlockSpec((tm, width), lambda i: (i, 0))
    return pl.pallas_call(
        _merge_out_kernel,
        grid=(n // tm,),
        in_specs=[row(D_MODEL), row(SB_WIDTH), row(ML_WIDTH), row(2 * D_MODEL),
                  _resident((SB_WIDTH, D_MODEL)), _resident((ML_WIDTH, D_MODEL)),
                  _resident((D_MODEL, D_MODEL)), _resident((1, D_MODEL))],
        out_specs=row(D_MODEL),
        out_shape=jax.ShapeDtypeStruct((n, D_MODEL), f32),
        compiler_params=pltpu.CompilerParams(dimension_semantics=("arbitrary",),
                                             vmem_limit_bytes=VMEM_LIMIT),
        name="merge_out",
    )(x2d, y_sb, y_ml, gates, w_sb, w_ml, w_out, g)


def _mlp_kernel(x_ref, gpre_ref, wup_ref, wdown_ref, gpost_ref, out_ref):
    x = x_ref[...]
    u = jnp.maximum(_mm(_rmsnorm(x, gpre_ref[...]).astype(bf16), wup_ref[...]), 0.0)
    down = _mm((u * u).astype(bf16), wdown_ref[...])
    out_ref[...] = x + _rmsnorm(down, gpost_ref[...])


def _mlp(x2d, g_pre, w_up, w_down, g_post):
    n = x2d.shape[0]
    tm = TOKEN_TILE
    row = pl.BlockSpec((tm, D_MODEL), lambda i: (i, 0))
    return pl.pallas_call(
        _mlp_kernel,
        grid=(n // tm,),
        in_specs=[row, _resident((1, D_MODEL)), _resident((D_MODEL, D_FF)),
                  _resident((D_FF, D_MODEL)), _resident((1, D_MODEL))],
        out_specs=row,
        out_shape=jax.ShapeDtypeStruct((n, D_MODEL), f32),
        compiler_params=pltpu.CompilerParams(dimension_semantics=("arbitrary",),
                                             vmem_limit_bytes=VMEM_LIMIT),
        name="mlp",
    )(x2d, g_pre, w_up, w_down, g_post)


def _ple_kernel(x_ref, p_ref, gpre_ref, wgate_ref, wproj_ref, gpost_ref, out_ref):
    x = x_ref[...]
    gate = jax.nn.sigmoid(_mm(_rmsnorm(x, gpre_ref[...]).astype(bf16), wgate_ref[...]))
    emb = _mm(p_ref[...].astype(bf16), wproj_ref[...])
    out_ref[...] = x + _rmsnorm(gate * emb, gpost_ref[...])


def _ple(x2d, p2d, g_pre, w_gate, w_proj, g_post):
    n = x2d.shape[0]
    tm = TOKEN_TILE
    row = lambda width: pl.BlockSpec((tm, width), lambda i: (i, 0))
    return pl.pallas_call(
        _ple_kernel,
        grid=(n // tm,),
        in_specs=[row(D_MODEL), row(PLE_DIM), _resident((1, D_MODEL)),
                  _resident((D_MODEL, D_MODEL)), _resident((PLE_DIM, D_MODEL)),
                  _resident((1, D_MODEL))],
        out_specs=row(D_MODEL),
        out_shape=jax.ShapeDtypeStruct((n, D_MODEL), f32),
        compiler_params=pltpu.CompilerParams(dimension_semantics=("arbitrary",),
                                             vmem_limit_bytes=VMEM_LIMIT),
        name="ple",
    )(x2d, p2d, g_pre, w_gate, w_proj, g_post)


def _reorder_in_proj_weight(w):
    gates_from = COL_GATE + 2 * ML_HEADS
    return jnp.concatenate(
        [w[:, :COL_GATE], w[:, gates_from:], w[:, COL_GATE:gates_from],
         jnp.zeros((w.shape[0], LANES - 2 * ML_HEADS), w.dtype)], axis=1).astype(bf16)


def kernel(x, p, g_mix_pre, w_in, b_igate, b_fgate, w_conv, b_conv, g_mlstm_head,
           w_branch_sb, w_branch_ml, w_out, g_mix_post, g_mlp_pre, w_mlp_up, w_mlp_down,
           g_mlp_post, g_ple_pre, w_ple_gate, w_ple_proj, g_ple_post):
    bsz, seq, d = x.shape
    n = bsz * seq
    x2d = x.reshape(n, d)
    vec = lambda a: a.reshape(1, -1)
    for layer in range(w_in.shape[0]):
        qkv_sb, u_ml, v_ml, o_ml, gates, i_f = _in_proj(
            x2d, vec(g_mix_pre[layer]), _reorder_in_proj_weight(w_in[layer]))

        y_sb = _sb_attn(qkv_sb.reshape(bsz, seq, -1))

        gate_bias = jnp.concatenate(
            [b_igate[layer], b_fgate[layer], jnp.zeros((LANES - 2 * ML_HEADS,), f32)])
        y_ml = _mlstm(u_ml.reshape(bsz, seq, -1), v_ml.reshape(bsz, seq, -1),
                      o_ml.reshape(bsz, seq, -1), i_f.reshape(bsz, seq, -1),
                      w_conv[layer], vec(b_conv[layer]), vec(gate_bias),
                      vec(g_mlstm_head[layer]))

        x2d = _merge_out(x2d, y_sb.reshape(n, -1), y_ml.reshape(n, -1), gates,
                         w_branch_sb[layer].astype(bf16), w_branch_ml[layer].astype(bf16),
                         w_out[layer].astype(bf16), vec(g_mix_post[layer]))
        x2d = _mlp(x2d, vec(g_mlp_pre[layer]), w_mlp_up[layer].astype(bf16),
                   w_mlp_down[layer].astype(bf16), vec(g_mlp_post[layer]))
        x2d = _ple(x2d, p[layer].reshape(n, -1), vec(g_ple_pre[layer]),
                   w_ple_gate[layer].astype(bf16), w_ple_proj[layer].astype(bf16),
                   vec(g_ple_post[layer]))
    return x2d.reshape(bsz, seq, d)
```

```python
import math
from functools import partial

import jax
import jax.numpy as jnp
from jax import lax
from jax.experimental import pallas as pl
from jax.experimental.pallas import tpu as pltpu

D_MODEL = 1024
PLE_DIM = 256
SB_HEADS = 8
SB_HEAD_DIM = 64
SB_WIDTH = SB_HEADS * SB_HEAD_DIM
SB_SCALE = 1.0 / math.sqrt(SB_HEAD_DIM)
ML_HEADS = 4
ML_HEAD_DIM = 128
ML_WIDTH = ML_HEADS * ML_HEAD_DIM
CONV_WIDTH = 4
D_FF = 4 * D_MODEL
RMS_EPS = 1e-6

LANES = 128
SUBLANES = 8
VMEM_LIMIT = 52 * 1024 * 1024

COL_SB = 0
COL_U = 3 * SB_WIDTH
COL_V = COL_U + 2 * ML_WIDTH
COL_O = COL_V + ML_WIDTH
COL_GATE = COL_O + ML_WIDTH
COL_IF = COL_GATE + 2 * D_MODEL
IN_COLS_PADDED = COL_IF + LANES

TOKEN_TILE = 256
SB_TILE = 256
ML_CHUNK = 256
MASKED = -1e30
LOG2E = math.log2(math.e)

f32 = jnp.float32
bf16 = jnp.bfloat16


def _rmsnorm(x, g):
    return x * lax.rsqrt(jnp.mean(x * x, axis=-1, keepdims=True) + RMS_EPS) * g


def _mm(a, b):
    return jnp.dot(a, b, preferred_element_type=f32)


def _mm_nt(a, b):
    return lax.dot_general(a, b, (((1,), (1,)), ((), ())), preferred_element_type=f32)


def _resident(shape):
    return pl.BlockSpec(shape, lambda *_: (0,) * len(shape), pipeline_mode=pl.Buffered(1))


def _in_proj_kernel(x_ref, g_ref, w_ref, sb_ref, u_ref, v_ref, o_ref, gate_ref, if_ref):
    h = _rmsnorm(x_ref[...], g_ref[...]).astype(bf16)

    def proj(lo, hi):
        return _mm(h, w_ref[:, lo:hi])

    sb_ref[:, 0:SB_WIDTH] = (proj(COL_SB, COL_SB + SB_WIDTH) * SB_SCALE).astype(bf16)
    sb_ref[:, SB_WIDTH:] = proj(COL_SB + SB_WIDTH, COL_U).astype(bf16)
    u_ref[...] = proj(COL_U, COL_V)
    v_ref[...] = proj(COL_V, COL_O).astype(bf16)
    o_ref[...] = proj(COL_O, COL_GATE)
    gate_ref[...] = proj(COL_GATE, COL_IF)
    if_ref[...] = proj(COL_IF, IN_COLS_PADDED)


def _in_proj(x2d, g, w):
    n = x2d.shape[0]
    tm = TOKEN_TILE
    row = lambda width: pl.BlockSpec((tm, width), lambda i: (i, 0))
    return pl.pallas_call(
        _in_proj_kernel,
        grid=(n // tm,),
        in_specs=[row(D_MODEL), _resident((1, D_MODEL)), _resident((D_MODEL, IN_COLS_PADDED))],
        out_specs=[row(3 * SB_WIDTH), row(2 * ML_WIDTH), row(ML_WIDTH), row(ML_WIDTH),
                   row(2 * D_MODEL), row(LANES)],
        out_shape=[jax.ShapeDtypeStruct((n, 3 * SB_WIDTH), bf16),
                   jax.ShapeDtypeStruct((n, 2 * ML_WIDTH), f32),
                   jax.ShapeDtypeStruct((n, ML_WIDTH), bf16),
                   jax.ShapeDtypeStruct((n, ML_WIDTH), f32),
                   jax.ShapeDtypeStruct((n, 2 * D_MODEL), f32),
                   jax.ShapeDtypeStruct((n, LANES), f32)],
        compiler_params=pltpu.CompilerParams(dimension_semantics=("arbitrary",),
                                             vmem_limit_bytes=VMEM_LIMIT),
        name="in_proj",
    )(x2d, g, w)


def _sb_attn_kernel(q_ref, k_ref, v_ref, o_ref, qh_ref, vh_ref, z_ref, cost_ref, logit_ref,
                    a_ref, carry_ref, acc_ref, bias_ref):
    t = SB_TILE
    n_q = q_ref.shape[0] // t
    n_items = n_q * (n_q + 1) // 2
    row = lax.broadcasted_iota(jnp.int32, (t, t), 0)
    col = lax.broadcasted_iota(jnp.int32, (t, t), 1)
    later = (row > col).astype(bf16)
    bias_ref[0] = jnp.zeros((t, t), f32)
    bias_ref[1] = jnp.where(col < row, 0.0, MASKED)
    carry_ref[...] = jnp.zeros_like(carry_ref)
    acc_ref[...] = jnp.zeros_like(acc_ref)
    first_head = lax.broadcasted_iota(jnp.int32, q_ref.shape, 1) < SB_HEAD_DIM
    for src_ref, dst_ref in ((q_ref, qh_ref), (v_ref, vh_ref)):
        x = src_ref[...]
        dst_ref[0] = jnp.where(first_head, x, jnp.zeros_like(x))
        dst_ref[1] = jnp.where(first_head, jnp.zeros_like(x), x)

    def rows(ref, tile):
        return ref.at[pl.ds(pl.multiple_of(tile * t, t), t), :]

    def scores(qi, kj):
        k2 = rows(k_ref, kj)[...]
        for h in range(2):
            z_ref[h] = _mm_nt(rows(qh_ref.at[h], qi)[...], k2)

    def costs(qi, kj):
        diagonal = kj == qi
        bias = bias_ref[diagonal.astype(jnp.int32)]
        for h in range(2):
            z = z_ref[h] + bias
            cost = jnp.maximum(z, 0.0) + jnp.log(1.0 + jnp.exp2(jnp.abs(z) * -LOG2E))
            carry = jnp.where(diagonal, 0.0, carry_ref[h])
            logit_ref[h] = (z - cost) - jnp.concatenate([carry] * (t // LANES), axis=1)
            cost_ref[h] = cost.astype(bf16)
            total = jnp.sum(cost, axis=1, keepdims=True)
            carry_ref[h] = carry + jnp.broadcast_to(total, (t, LANES))

    def weights(qi, kj):
        for h in range(2):
            after = _mm(cost_ref[h], later)
            a_ref[h] = jnp.exp(logit_ref[h] - after).astype(bf16)

    def outputs(qi, kj):
        pv = _mm(a_ref[0], rows(vh_ref.at[0], kj)[...]) + _mm(a_ref[1], rows(vh_ref.at[1], kj)[...])
        acc = jnp.where(kj == qi, pv, acc_ref[...] + pv)
        acc_ref[...] = acc
        rows(o_ref, qi)[...] = acc.astype(o_ref.dtype)

    stages = (outputs, weights, costs, scores)
    depth = len(stages)

    def following(item):
        qi, kj = item
        wrap = kj == 0
        return jnp.where(wrap, qi + 1, qi), jnp.where(wrap, qi + 1, kj - 1)

    def run(items, active):
        for stage, item, on in zip(stages, items, active):
            if on:
                stage(*item)

    items = [(jnp.int32(0), jnp.int32(0))] * depth
    for w in range(depth - 1):
        run(items, [i >= depth - 1 - w for i in range(depth)])
        items = items[1:] + [following(items[-1])]

    def step(_, flat):
        items = [tuple(flat[2 * i:2 * i + 2]) for i in range(depth)]
        run(items, [True] * depth)
        items = items[1:] + [following(items[-1])]
        return tuple(x for item in items for x in item)

    flat = lax.fori_loop(depth - 1, n_items, step, tuple(x for item in items for x in item))
    items = [tuple(flat[2 * i:2 * i + 2]) for i in range(depth)]
    for w in range(1, depth):
        run(items, [i < depth - w for i in range(depth)])
        items = items[1:] + [items[-1]]


def _sb_attn(qkv):
    bsz, seq, _ = qkv.shape
    pairs = SB_WIDTH // LANES
    spec = lambda off: pl.BlockSpec((None, seq, LANES), lambda b, hp: (b, 0, off + hp))
    return pl.pallas_call(
        _sb_attn_kernel,
        grid=(bsz, pairs),
        in_specs=[spec(0), spec(pairs), spec(2 * pairs)],
        out_specs=spec(0),
        out_shape=jax.ShapeDtypeStruct((bsz, seq, SB_WIDTH), bf16),
        scratch_shapes=[pltpu.VMEM((2, seq, LANES), bf16),
                        pltpu.VMEM((2, seq, LANES), bf16),
                        pltpu.VMEM((2, SB_TILE, SB_TILE), f32),
                        pltpu.VMEM((2, SB_TILE, SB_TILE), bf16),
                        pltpu.VMEM((2, SB_TILE, SB_TILE), f32),
                        pltpu.VMEM((2, SB_TILE, SB_TILE), bf16),
                        pltpu.VMEM((2, SB_TILE, LANES), f32),
                        pltpu.VMEM((SB_TILE, LANES), f32),
                        pltpu.VMEM((2, SB_TILE, SB_TILE), f32)],
        compiler_params=pltpu.CompilerParams(dimension_semantics=("arbitrary", "arbitrary"),
                                             vmem_limit_bytes=VMEM_LIMIT),
        name="sb_attn",
    )(qkv, qkv, qkv)


def _split3(x):
    hi = x.astype(bf16)
    r1 = x - hi.astype(f32)
    mid = r1.astype(bf16)
    lo = (r1 - mid.astype(f32)).astype(bf16)
    return hi, mid, lo


def _mlstm_kernel(u_ref, v_ref, o_ref, if_ref, wconv_ref, bconv_ref, bias_ref, ghead_ref,
                  y_ref, state_ref, m_ref, tail_ref):
    n = ML_CHUNK
    dh = ML_HEAD_DIM

    @pl.when(pl.program_id(1) == 0)
    def _():
        state_ref[...] = jnp.zeros_like(state_ref)
        m_ref[...] = jnp.zeros_like(m_ref)
        tail_ref[...] = jnp.zeros_like(tail_ref)

    u = u_ref[...]
    tail = tail_ref[...]
    row8 = lax.broadcasted_iota(jnp.int32, tail.shape, 0)
    acc = u * wconv_ref[CONV_WIDTH - 1:CONV_WIDTH, :] + bconv_ref[...]
    for back in range(1, CONV_WIDTH):
        rolled = pltpu.roll(u, back, 0)
        top = jnp.where(row8 < back, pltpu.roll(tail, back, 0), rolled[0:SUBLANES])
        shifted = jnp.concatenate([top, rolled[SUBLANES:]], axis=0)
        acc = acc + shifted * wconv_ref[CONV_WIDTH - 1 - back:CONV_WIDTH - back, :]
    tail_ref[...] = u[n - SUBLANES:n]
    qk = acc * jax.nn.sigmoid(acc)

    row = lax.broadcasted_iota(jnp.int32, (n, n), 0)
    col = lax.broadcasted_iota(jnp.int32, (n, n), 1)
    not_after = row >= col
    pre = if_ref[...] + bias_ref[...]
    pre_t = pre.T[0:2 * SUBLANES]
    log_f = jax.nn.log_sigmoid(pre)
    log_f_t = jax.nn.log_sigmoid(pre_t)
    lower = not_after.astype(bf16)
    upper = (row <= col).astype(bf16)
    b_cols = sum(_mm(lower, part) for part in _split3(log_f))
    b_rows = sum(_mm(part, upper) for part in _split3(log_f_t))

    ones_col = (lax.broadcasted_iota(jnp.int32, (n, dh), 1) == 0).astype(bf16)

    for h in range(ML_HEADS):
        q = qk[:, h * dh:(h + 1) * dh].astype(bf16)
        k = qk[:, ML_WIDTH + h * dh:ML_WIDTH + (h + 1) * dh] * (dh ** -0.5)
        v_ext = jnp.concatenate([v_ref[:, h * dh:(h + 1) * dh], ones_col], axis=1)
        b_col = b_cols[:, ML_HEADS + h:ML_HEADS + h + 1]
        i_col = pre[:, h:h + 1]
        b_row = b_rows[ML_HEADS + h:ML_HEADS + h + 1, :]
        i_row = pre_t[h:h + 1, :]
        m_prev = m_ref[h][0:1, 0:1]
        state = state_ref[h]

        d = jnp.where(not_after, b_col - (b_row - i_row), MASKED)
        m_inter = b_col + m_prev
        m_t = jnp.maximum(m_inter, jnp.max(d, axis=1, keepdims=True))
        s_mat = _mm_nt(q, k.astype(bf16)) * jnp.exp(d - m_t)
        inter = jnp.exp(m_inter - m_t)
        numden = _mm(s_mat.astype(bf16), v_ext) + inter * _mm(q, state.astype(bf16))
        den = numden[:, dh:dh + 1]
        hid = numden[:, 0:dh] * (1.0 / jnp.maximum(jnp.abs(den), jnp.exp(-m_t)))

        hid = hid * lax.rsqrt(jnp.mean(hid * hid, axis=-1, keepdims=True) + RMS_EPS)
        hid = hid * ghead_ref[:, h * dh:(h + 1) * dh]
        y = hid * jax.nn.sigmoid(o_ref[:, h * dh:(h + 1) * dh])
        y_ref[:, h * dh:(h + 1) * dh] = y.astype(y_ref.dtype)

        b_end = b_col[n - 1:n, :]
        g = b_end - b_col + i_col
        m_new = jnp.maximum(b_end + m_prev, jnp.max(g, axis=0, keepdims=True))
        w_s = jnp.exp(g - m_new)
        decay = jnp.exp(b_end + m_prev - m_new)
        wv = (w_s * v_ext.astype(f32)).astype(bf16)
        state_ref[h] = decay * state + _mm(k.T.astype(bf16), wv)
        m_ref[h] = jnp.broadcast_to(m_new, (SUBLANES, LANES))


def _mlstm(u, v, o, i_f, w_conv, b_conv, gate_bias, g_head):
    bsz, seq, _ = u.shape
    n = ML_CHUNK
    tok = lambda width: pl.BlockSpec((None, n, width), lambda b, c: (b, c, 0))
    return pl.pallas_call(
        _mlstm_kernel,
        grid=(bsz, seq // n),
        in_specs=[tok(2 * ML_WIDTH), tok(ML_WIDTH), tok(ML_WIDTH), tok(LANES),
                  _resident((CONV_WIDTH, 2 * ML_WIDTH)), _resident((1, 2 * ML_WIDTH)),
                  _resident((1, LANES)), _resident((1, ML_WIDTH))],
        out_specs=tok(ML_WIDTH),
        out_shape=jax.ShapeDtypeStruct((bsz, seq, ML_WIDTH), bf16),
        scratch_shapes=[pltpu.VMEM((ML_HEADS, ML_HEAD_DIM, 2 * ML_HEAD_DIM), f32),
                        pltpu.VMEM((ML_HEADS, SUBLANES, LANES), f32),
                        pltpu.VMEM((SUBLANES, 2 * ML_WIDTH), f32)],
        compiler_params=pltpu.CompilerParams(dimension_semantics=("arbitrary", "arbitrary"),
                                             vmem_limit_bytes=VMEM_LIMIT),
        name="mlstm",
    )(u, v, o, i_f, w_conv, b_conv, gate_bias, g_head)


def _post_mix_kernel(x_ref, ysb_ref, yml_ref, gate_ref, p_ref, wsb_ref, wml_ref, wout_ref,
                     wup_ref, wdown_ref, wgate_ref, wproj_ref, gains_ref, out_ref):
    gain = lambda i: gains_ref[i:i + 1, :]
    gate_sb = jax.nn.sigmoid(gate_ref[:, 0:D_MODEL])
    gate_ml = jax.nn.sigmoid(gate_ref[:, D_MODEL:])
    merged = gate_sb * _mm(ysb_ref[...], wsb_ref[...]) + gate_ml * _mm(yml_ref[...], wml_ref[...])
    x = x_ref[...] + _rmsnorm(_mm(merged.astype(bf16), wout_ref[...]), gain(0))

    u = jnp.maximum(_mm(_rmsnorm(x, gain(1)).astype(bf16), wup_ref[...]), 0.0)
    x = x + _rmsnorm(_mm((u * u).astype(bf16), wdown_ref[...]), gain(2))

    gate = jax.nn.sigmoid(_mm(_rmsnorm(x, gain(3)).astype(bf16), wgate_ref[...]))
    emb = _mm(p_ref[...].astype(bf16), wproj_ref[...])
    out_ref[...] = x + _rmsnorm(gate * emb, gain(4))


def _post_mix(x2d, y_sb, y_ml, gates, p2d, w_sb, w_ml, w_out, w_up, w_down, w_gate, w_proj, gains):
    n = x2d.shape[0]
    tm = TOKEN_TILE
    row = lambda width: pl.BlockSpec((tm, width), lambda i: (i, 0))
    weights = (w_sb, w_ml, w_out, w_up, w_down, w_gate, w_proj, gains)
    return pl.pallas_call(
        _post_mix_kernel,
        grid=(n // tm,),
        in_specs=[row(D_MODEL), row(SB_WIDTH), row(ML_WIDTH), row(2 * D_MODEL), row(PLE_DIM)]
                 + [_resident(w.shape) for w in weights],
        out_specs=row(D_MODEL),
        out_shape=jax.ShapeDtypeStruct((n, D_MODEL), f32),
        compiler_params=pltpu.CompilerParams(dimension_semantics=("arbitrary",),
                                             vmem_limit_bytes=VMEM_LIMIT),
        name="post_mix",
    )(x2d, y_sb, y_ml, gates, p2d, *weights)


def _reorder_in_proj_weight(w):
    gates_from = COL_GATE + 2 * ML_HEADS
    return jnp.concatenate(
        [w[:, :COL_GATE], w[:, gates_from:], w[:, COL_GATE:gates_from],
         jnp.zeros((w.shape[0], LANES - 2 * ML_HEADS), w.dtype)], axis=1).astype(bf16)


def kernel(x, p, g_mix_pre, w_in, b_igate, b_fgate, w_conv, b_conv, g_mlstm_head,
           w_branch_sb, w_branch_ml, w_out, g_mix_post, g_mlp_pre, w_mlp_up, w_mlp_down,
           g_mlp_post, g_ple_pre, w_ple_gate, w_ple_proj, g_ple_post):
    bsz, seq, d = x.shape
    n = bsz * seq
    x2d = x.reshape(n, d)
    vec = lambda a: a.reshape(1, -1)
    for layer in range(w_in.shape[0]):
        qkv_sb, u_ml, v_ml, o_ml, gates, i_f = _in_proj(
            x2d, vec(g_mix_pre[layer]), _reorder_in_proj_weight(w_in[layer]))

        y_sb = _sb_attn(qkv_sb.reshape(bsz, seq, -1))

        gate_bias = jnp.concatenate(
            [b_igate[layer], b_fgate[layer], jnp.zeros((LANES - 2 * ML_HEADS,), f32)])
        y_ml = _mlstm(u_ml.reshape(bsz, seq, -1), v_ml.reshape(bsz, seq, -1),
                      o_ml.reshape(bsz, seq, -1), i_f.reshape(bsz, seq, -1),
                      w_conv[layer], vec(b_conv[layer]), vec(gate_bias),
                      vec(g_mlstm_head[layer]))

        gains = jnp.stack([g_mix_post[layer], g_mlp_pre[layer], g_mlp_post[layer],
                           g_ple_pre[layer], g_ple_post[layer]])
        x2d = _post_mix(x2d, y_sb.reshape(n, -1), y_ml.reshape(n, -1), gates,
                        p[layer].reshape(n, -1),
                        w_branch_sb[layer].astype(bf16), w_branch_ml[layer].astype(bf16),
                        w_out[layer].astype(bf16), w_mlp_up[layer].astype(bf16),
                        w_mlp_down[layer].astype(bf16), w_ple_gate[layer].astype(bf16),
                        w_ple_proj[layer].astype(bf16), gains)
    return x2d.reshape(bsz, seq, d)
```

```python
import math

import jax
import jax.numpy as jnp
from jax import lax
from jax.experimental import pallas as pl
from jax.experimental.pallas import tpu as pltpu

D_MODEL = 1024
PLE_DIM = 256
SB_HEADS = 8
SB_HEAD_DIM = 64
SB_WIDTH = SB_HEADS * SB_HEAD_DIM
SB_SCALE = 1.0 / math.sqrt(SB_HEAD_DIM)
ML_HEADS = 4
ML_HEAD_DIM = 128
ML_WIDTH = ML_HEADS * ML_HEAD_DIM
CONV_WIDTH = 4
D_FF = 4 * D_MODEL
RMS_EPS = 1e-6

LANES = 128
SUBLANES = 8
VMEM_LIMIT = 52 * 1024 * 1024

COL_SB = 0
COL_U = 3 * SB_WIDTH
COL_V = COL_U + 2 * ML_WIDTH
COL_O = COL_V + ML_WIDTH
COL_GATE = COL_O + ML_WIDTH
COL_IF = COL_GATE + 2 * D_MODEL
IN_COLS_PADDED = COL_IF + LANES

IN_PROJ_TILE = 512
TOKEN_TILE = 256
SB_TILE = 256
ML_CHUNK = 256
MASKED = -1e30
LOG2E = math.log2(math.e)

f32 = jnp.float32
bf16 = jnp.bfloat16


def _rmsnorm(x, g):
    return x * lax.rsqrt(jnp.mean(x * x, axis=-1, keepdims=True) + RMS_EPS) * g


def _mm(a, b):
    return jnp.dot(a, b, preferred_element_type=f32)


def _mm_nt(a, b):
    return lax.dot_general(a, b, (((1,), (1,)), ((), ())), preferred_element_type=f32)


def _resident(shape):
    return pl.BlockSpec(shape, lambda *_: (0,) * len(shape), pipeline_mode=pl.Buffered(1))


def _in_proj_kernel(x_ref, g_ref, w_ref, sb_ref, u_ref, v_ref, o_ref, gate_ref, if_ref):
    h = _rmsnorm(x_ref[...], g_ref[...]).astype(bf16)

    def proj(lo, hi):
        return _mm(h, w_ref[:, lo:hi])

    sb_ref[:, 0:SB_WIDTH] = (proj(COL_SB, COL_SB + SB_WIDTH) * SB_SCALE).astype(bf16)
    sb_ref[:, SB_WIDTH:] = proj(COL_SB + SB_WIDTH, COL_U).astype(bf16)
    u_ref[...] = proj(COL_U, COL_V)
    v_ref[...] = proj(COL_V, COL_O).astype(bf16)
    o_ref[...] = proj(COL_O, COL_GATE)
    gate_ref[...] = proj(COL_GATE, COL_IF)
    if_ref[...] = proj(COL_IF, IN_COLS_PADDED)


def _in_proj(x2d, g, w):
    n = x2d.shape[0]
    tm = IN_PROJ_TILE
    row = lambda width: pl.BlockSpec((tm, width), lambda i: (i, 0))
    return pl.pallas_call(
        _in_proj_kernel,
        grid=(n // tm,),
        in_specs=[row(D_MODEL), _resident((1, D_MODEL)), _resident((D_MODEL, IN_COLS_PADDED))],
        out_specs=[row(3 * SB_WIDTH), row(2 * ML_WIDTH), row(ML_WIDTH), row(ML_WIDTH),
                   row(2 * D_MODEL), row(LANES)],
        out_shape=[jax.ShapeDtypeStruct((n, 3 * SB_WIDTH), bf16),
                   jax.ShapeDtypeStruct((n, 2 * ML_WIDTH), f32),
                   jax.ShapeDtypeStruct((n, ML_WIDTH), bf16),
                   jax.ShapeDtypeStruct((n, ML_WIDTH), f32),
                   jax.ShapeDtypeStruct((n, 2 * D_MODEL), f32),
                   jax.ShapeDtypeStruct((n, LANES), f32)],
        compiler_params=pltpu.CompilerParams(dimension_semantics=("arbitrary",),
                                             vmem_limit_bytes=VMEM_LIMIT),
        name="in_proj",
    )(x2d, g, w)


def _sb_attn_kernel(q_ref, k_ref, v_ref, o_ref, qh_ref, vh_ref, z_ref, cost_ref, logit_ref,
                    a_ref, carry_ref, acc_ref):
    t = SB_TILE
    n_q = q_ref.shape[0] // t
    n_items = n_q * (n_q + 1) // 2
    row = lax.broadcasted_iota(jnp.int32, (t, t), 0)
    col = lax.broadcasted_iota(jnp.int32, (t, t), 1)
    later = (row > col).astype(bf16)
    acc_ref[...] = jnp.zeros_like(acc_ref)
    x = v_ref[...]
    first_head = lax.broadcasted_iota(jnp.int32, x.shape, 1) < SB_HEAD_DIM
    vh_ref[0] = jnp.where(first_head, x, jnp.zeros_like(x))
    vh_ref[1] = jnp.where(first_head, jnp.zeros_like(x), x)
    first_head = lax.broadcasted_iota(jnp.int32, (t, LANES), 1) < SB_HEAD_DIM
    for i in range(n_q):
        x = q_ref[i * t:(i + 1) * t, :]
        qh_ref[i, 0:t, :] = jnp.where(first_head, x, jnp.zeros_like(x))
        qh_ref[i, t:2 * t, :] = jnp.where(first_head, jnp.zeros_like(x), x)

    def head(ref, h):
        return ref.at[h * t:(h + 1) * t, :]

    def rows(ref, tile):
        return ref.at[pl.ds(pl.multiple_of(tile * t, t), t), :]

    def scores(qi, kj):
        z_ref[...] = _mm_nt(qh_ref[qi], rows(k_ref, kj)[...])

    def mask_diagonal(qi, kj):
        @pl.when(kj == qi)
        def _():
            carry_ref[...] = jnp.zeros_like(carry_ref)
            for h in range(2):
                head(z_ref, h)[...] = jnp.where(col < row, head(z_ref, h)[...], MASKED)

    def costs(qi, kj):
        for h in range(2):
            z = head(z_ref, h)[...]
            cost = jnp.maximum(z, 0.0) + jnp.log(1.0 + jnp.exp2(jnp.abs(z) * -LOG2E))
            carry = carry_ref[h]
            head(logit_ref, h)[...] = (z - cost) - jnp.concatenate([carry] * (t // LANES), axis=1)
            head(cost_ref, h)[...] = cost.astype(bf16)
            total = jnp.sum(cost, axis=1, keepdims=True)
            carry_ref[h] = carry + jnp.broadcast_to(total, (t, LANES))

    def weights(qi, kj):
        after = _mm(cost_ref[...], later)
        a_ref[...] = jnp.exp(logit_ref[...] - after).astype(bf16)

    def outputs(qi, kj):
        pv = (_mm(head(a_ref, 0)[...], rows(vh_ref.at[0], kj)[...])
              + _mm(head(a_ref, 1)[...], rows(vh_ref.at[1], kj)[...]))
        acc = jnp.where(kj == qi, pv, acc_ref[...] + pv)
        acc_ref[...] = acc
        rows(o_ref, qi)[...] = acc.astype(o_ref.dtype)

    stages = (outputs, weights, costs, scores)
    depth = len(stages)

    def following(item):
        qi, kj = item
        wrap = kj == 0
        return jnp.where(wrap, qi + 1, qi), jnp.where(wrap, qi + 1, kj - 1)

    def run(items, active):
        if active[stages.index(costs)]:
            mask_diagonal(*items[stages.index(costs)])
        for stage, item, on in zip(stages, items, active):
            if on:
                stage(*item)

    items = [(jnp.int32(0), jnp.int32(0))] * depth
    for w in range(depth - 1):
        run(items, [i >= depth - 1 - w for i in range(depth)])
        items = items[1:] + [following(items[-1])]

    def step(_, flat):
        items = [tuple(flat[2 * i:2 * i + 2]) for i in range(depth)]
        run(items, [True] * depth)
        items = items[1:] + [following(items[-1])]
        return tuple(x for item in items for x in item)

    flat = lax.fori_loop(depth - 1, n_items, step, tuple(x for item in items for x in item))
    items = [tuple(flat[2 * i:2 * i + 2]) for i in range(depth)]
    for w in range(1, depth):
        run(items, [i < depth - w for i in range(depth)])
        items = items[1:] + [items[-1]]


def _sb_attn(qkv):
    bsz, seq, _ = qkv.shape
    pairs = SB_WIDTH // LANES
    spec = lambda off: pl.BlockSpec((None, seq, LANES), lambda b, hp: (b, 0, off + hp))
    return pl.pallas_call(
        _sb_attn_kernel,
        grid=(bsz, pairs),
        in_specs=[spec(0), spec(pairs), spec(2 * pairs)],
        out_specs=spec(0),
        out_shape=jax.ShapeDtypeStruct((bsz, seq, SB_WIDTH), bf16),
        scratch_shapes=[pltpu.VMEM((seq // SB_TILE, 2 * SB_TILE, LANES), bf16),
                        pltpu.VMEM((2, seq, LANES), bf16),
                        pltpu.VMEM((2 * SB_TILE, SB_TILE), f32),
                        pltpu.VMEM((2 * SB_TILE, SB_TILE), bf16),
                        pltpu.VMEM((2 * SB_TILE, SB_TILE), f32),
                        pltpu.VMEM((2 * SB_TILE, SB_TILE), bf16),
                        pltpu.VMEM((2, SB_TILE, LANES), f32),
                        pltpu.VMEM((SB_TILE, LANES), f32)],
        compiler_params=pltpu.CompilerParams(dimension_semantics=("arbitrary", "arbitrary"),
                                             vmem_limit_bytes=VMEM_LIMIT),
        name="sb_attn",
    )(qkv, qkv, qkv)


def _split3(x):
    hi = x.astype(bf16)
    r1 = x - hi.astype(f32)
    mid = r1.astype(bf16)
    lo = (r1 - mid.astype(f32)).astype(bf16)
    return hi, mid, lo


def _mlstm_kernel(u_ref, v_ref, o_ref, if_ref, wconv_ref, bconv_ref, bias_ref, ghead_ref,
                  y_ref, state_ref, m_ref, tail_ref):
    n = ML_CHUNK
    dh = ML_HEAD_DIM

    @pl.when(pl.program_id(1) == 0)
    def _():
        state_ref[...] = jnp.zeros_like(state_ref)
        m_ref[...] = jnp.zeros_like(m_ref)
        tail_ref[...] = jnp.zeros_like(tail_ref)

    u = u_ref[...]
    tail = tail_ref[...]
    row8 = lax.broadcasted_iota(jnp.int32, tail.shape, 0)
    acc = u * wconv_ref[CONV_WIDTH - 1:CONV_WIDTH, :] + bconv_ref[...]
    for back in range(1, CONV_WIDTH):
        rolled = pltpu.roll(u, back, 0)
        top = jnp.where(row8 < back, pltpu.roll(tail, back, 0), rolled[0:SUBLANES])
        shifted = jnp.concatenate([top, rolled[SUBLANES:]], axis=0)
        acc = acc + shifted * wconv_ref[CONV_WIDTH - 1 - back:CONV_WIDTH - back, :]
    tail_ref[...] = u[n - SUBLANES:n]
    qk = acc * jax.nn.sigmoid(acc)

    row = lax.broadcasted_iota(jnp.int32, (n, n), 0)
    col = lax.broadcasted_iota(jnp.int32, (n, n), 1)
    not_after = row >= col
    pre = if_ref[...] + bias_ref[...]
    pre_t = pre.T[0:2 * SUBLANES]
    log_f = jax.nn.log_sigmoid(pre)
    log_f_t = jax.nn.log_sigmoid(pre_t)
    lower = not_after.astype(bf16)
    upper = (row <= col).astype(bf16)
    b_cols = sum(_mm(lower, part) for part in _split3(log_f))
    b_rows = sum(_mm(part, upper) for part in _split3(log_f_t))

    ones_col = (lax.broadcasted_iota(jnp.int32, (n, dh), 1) == 0).astype(bf16)

    for h in range(ML_HEADS):
        q = qk[:, h * dh:(h + 1) * dh].astype(bf16)
        k = qk[:, ML_WIDTH + h * dh:ML_WIDTH + (h + 1) * dh] * (dh ** -0.5)
        v_ext = jnp.concatenate([v_ref[:, h * dh:(h + 1) * dh], ones_col], axis=1)
        b_col = b_cols[:, ML_HEADS + h:ML_HEADS + h + 1]
        i_col = pre[:, h:h + 1]
        b_row = b_rows[ML_HEADS + h:ML_HEADS + h + 1, :]
        i_row = pre_t[h:h + 1, :]
        m_prev = m_ref[h][0:1, 0:1]
        state = state_ref[h]

        d = jnp.where(not_after, b_col - (b_row - i_row), MASKED)
        m_inter = b_col + m_prev
        m_t = jnp.maximum(m_inter, jnp.max(d, axis=1, keepdims=True))
        s_mat = _mm_nt(q, k.astype(bf16)) * jnp.exp(d - m_t)
        inter = jnp.exp(m_inter - m_t)
        numden = _mm(s_mat.astype(bf16), v_ext) + inter * _mm(q, state.astype(bf16))
        den = numden[:, dh:dh + 1]
        hid = numden[:, 0:dh] * (1.0 / jnp.maximum(jnp.abs(den), jnp.exp(-m_t)))

        hid = hid * lax.rsqrt(jnp.mean(hid * hid, axis=-1, keepdims=True) + RMS_EPS)
        hid = hid * ghead_ref[:, h * dh:(h + 1) * dh]
        y = hid * jax.nn.sigmoid(o_ref[:, h * dh:(h + 1) * dh])
        y_ref[:, h * dh:(h + 1) * dh] = y.astype(y_ref.dtype)

        b_end = b_col[n - 1:n, :]
        g = b_end - b_col + i_col
        m_new = jnp.maximum(b_end + m_prev, jnp.max(g, axis=0, keepdims=True))
        w_s = jnp.exp(g - m_new)
        decay = jnp.exp(b_end + m_prev - m_new)
        wv = (w_s * v_ext.astype(f32)).astype(bf16)
        state_ref[h] = decay * state + _mm(k.T.astype(bf16), wv)
        m_ref[h] = jnp.broadcast_to(m_new, (SUBLANES, LANES))


def _mlstm(u, v, o, i_f, w_conv, b_conv, gate_bias, g_head):
    bsz, seq, _ = u.shape
    n = ML_CHUNK
    tok = lambda width: pl.BlockSpec((None, n, width), lambda b, c: (b, c, 0))
    return pl.pallas_call(
        _mlstm_kernel,
        grid=(bsz, seq // n),
        in_specs=[tok(2 * ML_WIDTH), tok(ML_WIDTH), tok(ML_WIDTH), tok(LANES),
                  _resident((CONV_WIDTH, 2 * ML_WIDTH)), _resident((1, 2 * ML_WIDTH)),
                  _resident((1, LANES)), _resident((1, ML_WIDTH))],
        out_specs=tok(ML_WIDTH),
        out_shape=jax.ShapeDtypeStruct((bsz, seq, ML_WIDTH), bf16),
        scratch_shapes=[pltpu.VMEM((ML_HEADS, ML_HEAD_DIM, 2 * ML_HEAD_DIM), f32),
                        pltpu.VMEM((ML_HEADS, SUBLANES, LANES), f32),
                        pltpu.VMEM((SUBLANES, 2 * ML_WIDTH), f32)],
        compiler_params=pltpu.CompilerParams(dimension_semantics=("arbitrary", "arbitrary"),
                                             vmem_limit_bytes=VMEM_LIMIT),
        name="mlstm",
    )(u, v, o, i_f, w_conv, b_conv, gate_bias, g_head)


def _post_mix_kernel(x_ref, ysb_ref, yml_ref, gate_ref, p_ref, wsb_ref, wml_ref, wout_ref,
                     wup_ref, wdown_ref, wgate_ref, wproj_ref, gains_ref, out_ref):
    gain = lambda i: gains_ref[i:i + 1, :]
    gate_sb = jax.nn.sigmoid(gate_ref[:, 0:D_MODEL])
    gate_ml = jax.nn.sigmoid(gate_ref[:, D_MODEL:])
    merged = gate_sb * _mm(ysb_ref[...], wsb_ref[...]) + gate_ml * _mm(yml_ref[...], wml_ref[...])
    x = x_ref[...] + _rmsnorm(_mm(merged.astype(bf16), wout_ref[...]), gain(0))

    u = jnp.maximum(_mm(_rmsnorm(x, gain(1)).astype(bf16), wup_ref[...]), 0.0)
    x = x + _rmsnorm(_mm((u * u).astype(bf16), wdown_ref[...]), gain(2))

    gate = jax.nn.sigmoid(_mm(_rmsnorm(x, gain(3)).astype(bf16), wgate_ref[...]))
    emb = _mm(p_ref[...].astype(bf16), wproj_ref[...])
    out_ref[...] = x + _rmsnorm(gate * emb, gain(4))


def _post_mix(x2d, y_sb, y_ml, gates, p2d, w_sb, w_ml, w_out, w_up, w_down, w_gate, w_proj, gains):
    n = x2d.shape[0]
    tm = TOKEN_TILE
    row = lambda width: pl.BlockSpec((tm, width), lambda i: (i, 0))
    weights = (w_sb, w_ml, w_out, w_up, w_down, w_gate, w_proj, gains)
    return pl.pallas_call(
        _post_mix_kernel,
        grid=(n // tm,),
        in_specs=[row(D_MODEL), row(SB_WIDTH), row(ML_WIDTH), row(2 * D_MODEL), row(PLE_DIM)]
                 + [_resident(w.shape) for w in weights],
        out_specs=row(D_MODEL),
        out_shape=jax.ShapeDtypeStruct((n, D_MODEL), f32),
        compiler_params=pltpu.CompilerParams(dimension_semantics=("arbitrary",),
                                             vmem_limit_bytes=VMEM_LIMIT),
        name="post_mix",
    )(x2d, y_sb, y_ml, gates, p2d, *weights)


def _reorder_in_proj_weight(w):
    gates_from = COL_GATE + 2 * ML_HEADS
    w = w.astype(bf16)
    return jnp.concatenate(
        [w[:, :COL_GATE], w[:, gates_from:], w[:, COL_GATE:gates_from],
         jnp.zeros((w.shape[0], LANES - 2 * ML_HEADS), w.dtype)], axis=1)


def kernel(x, p, g_mix_pre, w_in, b_igate, b_fgate, w_conv, b_conv, g_mlstm_head,
           w_branch_sb, w_branch_ml, w_out, g_mix_post, g_mlp_pre, w_mlp_up, w_mlp_down,
           g_mlp_post, g_ple_pre, w_ple_gate, w_ple_proj, g_ple_post):
    bsz, seq, d = x.shape
    n = bsz * seq
    x2d = x.reshape(n, d)
    vec = lambda a: a.reshape(1, -1)
    for layer in range(w_in.shape[0]):
        qkv_sb, u_ml, v_ml, o_ml, gates, i_f = _in_proj(
            x2d, vec(g_mix_pre[layer]), _reorder_in_proj_weight(w_in[layer]))

        y_sb = _sb_attn(qkv_sb.reshape(bsz, seq, -1))

        gate_bias = jnp.concatenate(
            [b_igate[layer], b_fgate[layer], jnp.zeros((LANES - 2 * ML_HEADS,), f32)])
        y_ml = _mlstm(u_ml.reshape(bsz, seq, -1), v_ml.reshape(bsz, seq, -1),
                      o_ml.reshape(bsz, seq, -1), i_f.reshape(bsz, seq, -1),
                      w_conv[layer], vec(b_conv[layer]), vec(gate_bias),
                      vec(g_mlstm_head[layer]))

        gains = jnp.stack([g_mix_post[layer], g_mlp_pre[layer], g_mlp_post[layer],
                           g_ple_pre[layer], g_ple_post[layer]])
        x2d = _post_mix(x2d, y_sb.reshape(n, -1), y_ml.reshape(n, -1), gates,
                        p[layer].reshape(n, -1),
                        w_branch_sb[layer].astype(bf16), w_branch_ml[layer].astype(bf16),
                        w_out[layer].astype(bf16), w_mlp_up[layer].astype(bf16),
                        w_mlp_down[layer].astype(bf16), w_ple_gate[layer].astype(bf16),
                        w_ple_proj[layer].astype(bf16), gains)
    return x2d.reshape(bsz, seq, d)
```

```python
import math

import jax
import jax.numpy as jnp
from jax import lax
from jax.experimental import pallas as pl
from jax.experimental.pallas import tpu as pltpu

D_MODEL = 1024
PLE_DIM = 256
SB_HEADS = 8
SB_HEAD_DIM = 64
SB_WIDTH = SB_HEADS * SB_HEAD_DIM
SB_SCALE = 1.0 / math.sqrt(SB_HEAD_DIM)
ML_HEADS = 4
ML_HEAD_DIM = 128
ML_WIDTH = ML_HEADS * ML_HEAD_DIM
CONV_WIDTH = 4
D_FF = 4 * D_MODEL
RMS_EPS = 1e-6

LANES = 128
SUBLANES = 8
VMEM_LIMIT = 52 * 1024 * 1024

COL_SB = 0
COL_U = 3 * SB_WIDTH
COL_V = COL_U + 2 * ML_WIDTH
COL_O = COL_V + ML_WIDTH
COL_GATE = COL_O + ML_WIDTH
COL_IF = COL_GATE + 2 * D_MODEL
IN_COLS_PADDED = COL_IF + LANES

IN_PROJ_TILE = 512
TOKEN_TILE = 256
SB_TILE = 256
ML_CHUNK = 256
MASKED = -1e30
LOG2E = math.log2(math.e)

f32 = jnp.float32
bf16 = jnp.bfloat16


def _rmsnorm(x, g):
    return x * lax.rsqrt(jnp.mean(x * x, axis=-1, keepdims=True) + RMS_EPS) * g


def _mm(a, b):
    return jnp.dot(a, b, preferred_element_type=f32)


def _mm_nt(a, b):
    return lax.dot_general(a, b, (((1,), (1,)), ((), ())), preferred_element_type=f32)


def _resident(shape):
    return pl.BlockSpec(shape, lambda *_: (0,) * len(shape), pipeline_mode=pl.Buffered(1))


def _in_proj_kernel(x_ref, g_ref, w_ref, sb_ref, u_ref, v_ref, o_ref, gate_ref, if_ref):
    h = _rmsnorm(x_ref[...], g_ref[...]).astype(bf16)

    def proj(lo, hi):
        return _mm(h, w_ref[:, lo:hi])

    sb_ref[:, 0:SB_WIDTH] = (proj(COL_SB, COL_SB + SB_WIDTH) * SB_SCALE).astype(bf16)
    sb_ref[:, SB_WIDTH:] = proj(COL_SB + SB_WIDTH, COL_U).astype(bf16)
    u_ref[...] = proj(COL_U, COL_V)
    v_ref[...] = proj(COL_V, COL_O).astype(bf16)
    o_ref[...] = proj(COL_O, COL_GATE)
    gate_ref[...] = proj(COL_GATE, COL_IF)
    if_ref[...] = proj(COL_IF, IN_COLS_PADDED)


def _in_proj(x2d, g, w):
    n = x2d.shape[0]
    tm = IN_PROJ_TILE
    row = lambda width: pl.BlockSpec((tm, width), lambda i: (i, 0))
    return pl.pallas_call(
        _in_proj_kernel,
        grid=(n // tm,),
        in_specs=[row(D_MODEL), _resident((1, D_MODEL)), _resident((D_MODEL, IN_COLS_PADDED))],
        out_specs=[row(3 * SB_WIDTH), row(2 * ML_WIDTH), row(ML_WIDTH), row(ML_WIDTH),
                   row(2 * D_MODEL), row(LANES)],
        out_shape=[jax.ShapeDtypeStruct((n, 3 * SB_WIDTH), bf16),
                   jax.ShapeDtypeStruct((n, 2 * ML_WIDTH), f32),
                   jax.ShapeDtypeStruct((n, ML_WIDTH), bf16),
                   jax.ShapeDtypeStruct((n, ML_WIDTH), f32),
                   jax.ShapeDtypeStruct((n, 2 * D_MODEL), f32),
                   jax.ShapeDtypeStruct((n, LANES), f32)],
        compiler_params=pltpu.CompilerParams(dimension_semantics=("arbitrary",),
                                             vmem_limit_bytes=VMEM_LIMIT),
        name="in_proj",
    )(x2d, g, w)


def _sb_attn_kernel(q_ref, k_ref, v_ref, o_ref, qh_ref, vh_ref, z_ref, cost_ref, logit_ref,
                    a_ref, carry_ref, acc_ref):
    t = SB_TILE
    n_q = q_ref.shape[0] // t
    n_items = n_q * (n_q + 1) // 2
    row = lax.broadcasted_iota(jnp.int32, (t, t), 0)
    col = lax.broadcasted_iota(jnp.int32, (t, t), 1)
    later = (row > col).astype(bf16)
    acc_ref[...] = jnp.zeros_like(acc_ref)
    x = v_ref[...]
    first_head = lax.broadcasted_iota(jnp.int32, x.shape, 1) < SB_HEAD_DIM
    vh_ref[0] = jnp.where(first_head, x, jnp.zeros_like(x))
    vh_ref[1] = jnp.where(first_head, jnp.zeros_like(x), x)
    first_head = lax.broadcasted_iota(jnp.int32, (t, LANES), 1) < SB_HEAD_DIM
    for i in range(n_q):
        x = q_ref[i * t:(i + 1) * t, :]
        qh_ref[i, 0:t, :] = jnp.where(first_head, x, jnp.zeros_like(x))
        qh_ref[i, t:2 * t, :] = jnp.where(first_head, jnp.zeros_like(x), x)

    def head(ref, h):
        return ref.at[h * t:(h + 1) * t, :]

    def rows(ref, tile):
        return ref.at[pl.ds(pl.multiple_of(tile * t, t), t), :]

    def scores(qi, kj, p):
        z_ref[p] = _mm_nt(qh_ref[qi], rows(k_ref, kj)[...])

    def mask_diagonal(qi, kj, p):
        @pl.when(kj == qi)
        def _():
            carry_ref[...] = jnp.zeros_like(carry_ref)
            for h in range(2):
                z = head(z_ref.at[1 - p], h)
                z[...] = jnp.where(col < row, z[...], MASKED)

    def costs(qi, kj, p):
        for h in range(2):
            z = head(z_ref.at[1 - p], h)[...]
            cost = jnp.maximum(z, 0.0) + jnp.log(1.0 + jnp.exp2(jnp.abs(z) * -LOG2E))
            carry = carry_ref[h]
            head(logit_ref.at[p], h)[...] = (
                (z - cost) - jnp.concatenate([carry] * (t // LANES), axis=1))
            head(cost_ref.at[p], h)[...] = cost.astype(bf16)
            total = jnp.sum(cost, axis=1, keepdims=True)
            carry_ref[h] = carry + jnp.broadcast_to(total, (t, LANES))

    def weights(qi, kj, p):
        after = _mm(cost_ref[1 - p], later)
        a_ref[p] = jnp.exp(logit_ref[1 - p] - after).astype(bf16)

    def outputs(qi, kj, p):
        pv = (_mm(head(a_ref.at[1 - p], 0)[...], rows(vh_ref.at[0], kj)[...])
              + _mm(head(a_ref.at[1 - p], 1)[...], rows(vh_ref.at[1], kj)[...]))
        acc = jnp.where(kj == qi, pv, acc_ref[...] + pv)
        acc_ref[...] = acc
        rows(o_ref, qi)[...] = acc.astype(o_ref.dtype)

    stages = (outputs, weights, costs, scores)
    depth = len(stages)
    assert depth % 2 == 0 and (n_items - depth) % 2 == 0

    def following(item):
        qi, kj = item
        wrap = kj == 0
        return jnp.where(wrap, qi + 1, qi), jnp.where(wrap, qi + 1, kj - 1)

    def run(items, active, w):
        p = w % 2
        if active[stages.index(costs)]:
            mask_diagonal(*items[stages.index(costs)], p)
        for stage, item, on in zip(stages, items, active):
            if on:
                stage(*item, p)

    items = [(jnp.int32(0), jnp.int32(0))] * depth
    for w in range(depth - 1):
        run(items, [i >= depth - 1 - w for i in range(depth)], w)
        items = items[1:] + [following(items[-1])]

    def step(flat, w):
        items = [tuple(flat[2 * i:2 * i + 2]) for i in range(depth)]
        run(items, [True] * depth, w)
        items = items[1:] + [following(items[-1])]
        return tuple(x for item in items for x in item)

    flat = step(tuple(x for item in items for x in item), depth - 1)
    flat = lax.fori_loop(0, (n_items - depth) // 2, lambda _, f: step(step(f, 0), 1), flat)
    items = [tuple(flat[2 * i:2 * i + 2]) for i in range(depth)]
    for j in range(1, depth):
        run(items, [i < depth - j for i in range(depth)], n_items + j - 1)
        items = items[1:] + [items[-1]]


def _sb_attn(qkv):
    bsz, seq, _ = qkv.shape
    pairs = SB_WIDTH // LANES
    spec = lambda off: pl.BlockSpec((None, seq, LANES), lambda b, hp: (b, 0, off + hp))
    return pl.pallas_call(
        _sb_attn_kernel,
        grid=(bsz, pairs),
        in_specs=[spec(0), spec(pairs), spec(2 * pairs)],
        out_specs=spec(0),
        out_shape=jax.ShapeDtypeStruct((bsz, seq, SB_WIDTH), bf16),
        scratch_shapes=[pltpu.VMEM((seq // SB_TILE, 2 * SB_TILE, LANES), bf16),
                        pltpu.VMEM((2, seq, LANES), bf16),
                        pltpu.VMEM((2, 2 * SB_TILE, SB_TILE), f32),
                        pltpu.VMEM((2, 2 * SB_TILE, SB_TILE), bf16),
                        pltpu.VMEM((2, 2 * SB_TILE, SB_TILE), f32),
                        pltpu.VMEM((2, 2 * SB_TILE, SB_TILE), bf16),
                        pltpu.VMEM((2, SB_TILE, LANES), f32),
                        pltpu.VMEM((SB_TILE, LANES), f32)],
        compiler_params=pltpu.CompilerParams(dimension_semantics=("arbitrary", "arbitrary"),
                                             vmem_limit_bytes=VMEM_LIMIT),
        name="sb_attn",
    )(qkv, qkv, qkv)


def _split3(x):
    hi = x.astype(bf16)
    r1 = x - hi.astype(f32)
    mid = r1.astype(bf16)
    lo = (r1 - mid.astype(f32)).astype(bf16)
    return hi, mid, lo


def _mlstm_kernel(u_ref, v_ref, o_ref, if_ref, wconv_ref, bconv_ref, bias_ref, ghead_ref,
                  y_ref, state_ref, m_ref, tail_ref):
    n = ML_CHUNK
    dh = ML_HEAD_DIM

    @pl.when(pl.program_id(1) == 0)
    def _():
        state_ref[...] = jnp.zeros_like(state_ref)
        m_ref[...] = jnp.zeros_like(m_ref)
        tail_ref[...] = jnp.zeros_like(tail_ref)

    u = u_ref[...]
    tail = tail_ref[...]
    row8 = lax.broadcasted_iota(jnp.int32, tail.shape, 0)
    acc = u * wconv_ref[CONV_WIDTH - 1:CONV_WIDTH, :] + bconv_ref[...]
    for back in range(1, CONV_WIDTH):
        rolled = pltpu.roll(u, back, 0)
        top = jnp.where(row8 < back, pltpu.roll(tail, back, 0), rolled[0:SUBLANES])
        shifted = jnp.concatenate([top, rolled[SUBLANES:]], axis=0)
        acc = acc + shifted * wconv_ref[CONV_WIDTH - 1 - back:CONV_WIDTH - back, :]
    tail_ref[...] = u[n - SUBLANES:n]
    qk = acc * jax.nn.sigmoid(acc)

    row = lax.broadcasted_iota(jnp.int32, (n, n), 0)
    col = lax.broadcasted_iota(jnp.int32, (n, n), 1)
    not_after = row >= col
    pre = if_ref[...] + bias_ref[...]
    pre_t = pre.T[0:2 * SUBLANES]
    log_f = jax.nn.log_sigmoid(pre)
    log_f_t = jax.nn.log_sigmoid(pre_t)
    lower = not_after.astype(bf16)
    upper = (row <= col).astype(bf16)
    b_cols = sum(_mm(lower, part) for part in _split3(log_f))
    b_rows = sum(_mm(part, upper) for part in _split3(log_f_t))

    ones_col = (lax.broadcasted_iota(jnp.int32, (n, dh), 1) == 0).astype(bf16)

    for h in range(ML_HEADS):
        q = qk[:, h * dh:(h + 1) * dh].astype(bf16)
        k = qk[:, ML_WIDTH + h * dh:ML_WIDTH + (h + 1) * dh] * (dh ** -0.5)
        v_ext = jnp.concatenate([v_ref[:, h * dh:(h + 1) * dh], ones_col], axis=1)
        b_col = b_cols[:, ML_HEADS + h:ML_HEADS + h + 1]
        i_col = pre[:, h:h + 1]
        b_row = b_rows[ML_HEADS + h:ML_HEADS + h + 1, :]
        i_row = pre_t[h:h + 1, :]
        m_prev = m_ref[h][0:1, 0:1]
        state = state_ref[h]

        d = jnp.where(not_after, b_col - (b_row - i_row), MASKED)
        m_inter = b_col + m_prev
        m_t = jnp.maximum(m_inter, jnp.max(d, axis=1, keepdims=True))
        s_mat = _mm_nt(q, k.astype(bf16)) * jnp.exp(d - m_t)
        inter = jnp.exp(m_inter - m_t)
        numden = _mm(s_mat.astype(bf16), v_ext) + inter * _mm(q, state.astype(bf16))
        den = numden[:, dh:dh + 1]
        hid = numden[:, 0:dh] * (1.0 / jnp.maximum(jnp.abs(den), jnp.exp(-m_t)))

        hid = hid * lax.rsqrt(jnp.mean(hid * hid, axis=-1, keepdims=True) + RMS_EPS)
        hid = hid * ghead_ref[:, h * dh:(h + 1) * dh]
        y = hid * jax.nn.sigmoid(o_ref[:, h * dh:(h + 1) * dh])
        y_ref[:, h * dh:(h + 1) * dh] = y.astype(y_ref.dtype)

        b_end = b_col[n - 1:n, :]
        g = b_end - b_col + i_col
        m_new = jnp.maximum(b_end + m_prev, jnp.max(g, axis=0, keepdims=True))
        w_s = jnp.exp(g - m_new)
        decay = jnp.exp(b_end + m_prev - m_new)
        wv = (w_s * v_ext.astype(f32)).astype(bf16)
        state_ref[h] = decay * state + _mm(k.T.astype(bf16), wv)
        m_ref[h] = jnp.broadcast_to(m_new, (SUBLANES, LANES))


def _mlstm(u, v, o, i_f, w_conv, b_conv, gate_bias, g_head):
    bsz, seq, _ = u.shape
    n = ML_CHUNK
    tok = lambda width: pl.BlockSpec((None, n, width), lambda b, c: (b, c, 0))
    return pl.pallas_call(
        _mlstm_kernel,
        grid=(bsz, seq // n),
        in_specs=[tok(2 * ML_WIDTH), tok(ML_WIDTH), tok(ML_WIDTH), tok(LANES),
                  _resident((CONV_WIDTH, 2 * ML_WIDTH)), _resident((1, 2 * ML_WIDTH)),
                  _resident((1, LANES)), _resident((1, ML_WIDTH))],
        out_specs=tok(ML_WIDTH),
        out_shape=jax.ShapeDtypeStruct((bsz, seq, ML_WIDTH), bf16),
        scratch_shapes=[pltpu.VMEM((ML_HEADS, ML_HEAD_DIM, 2 * ML_HEAD_DIM), f32),
                        pltpu.VMEM((ML_HEADS, SUBLANES, LANES), f32),
                        pltpu.VMEM((SUBLANES, 2 * ML_WIDTH), f32)],
        compiler_params=pltpu.CompilerParams(dimension_semantics=("arbitrary", "arbitrary"),
                                             vmem_limit_bytes=VMEM_LIMIT),
        name="mlstm",
    )(u, v, o, i_f, w_conv, b_conv, gate_bias, g_head)


def _post_mix_kernel(x_ref, ysb_ref, yml_ref, gate_ref, p_ref, wsb_ref, wml_ref, wout_ref,
                     wup_ref, wdown_ref, wgate_ref, wproj_ref, gains_ref, out_ref):
    gain = lambda i: gains_ref[i:i + 1, :]
    gate_sb = jax.nn.sigmoid(gate_ref[:, 0:D_MODEL])
    gate_ml = jax.nn.sigmoid(gate_ref[:, D_MODEL:])
    merged = gate_sb * _mm(ysb_ref[...], wsb_ref[...]) + gate_ml * _mm(yml_ref[...], wml_ref[...])
    x = x_ref[...] + _rmsnorm(_mm(merged.astype(bf16), wout_ref[...]), gain(0))

    u = jnp.maximum(_mm(_rmsnorm(x, gain(1)).astype(bf16), wup_ref[...]), 0.0)
    x = x + _rmsnorm(_mm((u * u).astype(bf16), wdown_ref[...]), gain(2))

    gate = jax.nn.sigmoid(_mm(_rmsnorm(x, gain(3)).astype(bf16), wgate_ref[...]))
    emb = _mm(p_ref[...].astype(bf16), wproj_ref[...])
    out_ref[...] = x + _rmsnorm(gate * emb, gain(4))


def _post_mix(x2d, y_sb, y_ml, gates, p2d, w_sb, w_ml, w_out, w_up, w_down, w_gate, w_proj, gains):
    n = x2d.shape[0]
    tm = TOKEN_TILE
    row = lambda width: pl.BlockSpec((tm, width), lambda i: (i, 0))
    weights = (w_sb, w_ml, w_out, w_up, w_down, w_gate, w_proj, gains)
    return pl.pallas_call(
        _post_mix_kernel,
        grid=(n // tm,),
        in_specs=[row(D_MODEL), row(SB_WIDTH), row(ML_WIDTH), row(2 * D_MODEL), row(PLE_DIM)]
                 + [_resident(w.shape) for w in weights],
        out_specs=row(D_MODEL),
        out_shape=jax.ShapeDtypeStruct((n, D_MODEL), f32),
        compiler_params=pltpu.CompilerParams(dimension_semantics=("arbitrary",),
                                             vmem_limit_bytes=VMEM_LIMIT),
        name="post_mix",
    )(x2d, y_sb, y_ml, gates, p2d, *weights)


def _reorder_in_proj_weight(w):
    gates_from = COL_GATE + 2 * ML_HEADS
    w = w.astype(bf16)
    return jnp.concatenate(
        [w[:, :COL_GATE], w[:, gates_from:], w[:, COL_GATE:gates_from],
         jnp.zeros((w.shape[0], LANES - 2 * ML_HEADS), w.dtype)], axis=1)


def kernel(x, p, g_mix_pre, w_in, b_igate, b_fgate, w_conv, b_conv, g_mlstm_head,
           w_branch_sb, w_branch_ml, w_out, g_mix_post, g_mlp_pre, w_mlp_up, w_mlp_down,
           g_mlp_post, g_ple_pre, w_ple_gate, w_ple_proj, g_ple_post):
    bsz, seq, d = x.shape
    n = bsz * seq
    x2d = x.reshape(n, d)
    vec = lambda a: a.reshape(1, -1)
    for layer in range(w_in.shape[0]):
        qkv_sb, u_ml, v_ml, o_ml, gates, i_f = _in_proj(
            x2d, vec(g_mix_pre[layer]), _reorder_in_proj_weight(w_in[layer]))

        y_sb = _sb_attn(qkv_sb.reshape(bsz, seq, -1))

        gate_bias = jnp.concatenate(
            [b_igate[layer], b_fgate[layer], jnp.zeros((LANES - 2 * ML_HEADS,), f32)])
        y_ml = _mlstm(u_ml.reshape(bsz, seq, -1), v_ml.reshape(bsz, seq, -1),
                      o_ml.reshape(bsz, seq, -1), i_f.reshape(bsz, seq, -1),
                      w_conv[layer], vec(b_conv[layer]), vec(gate_bias),
                      vec(g_mlstm_head[layer]))

        gains = jnp.stack([g_mix_post[layer], g_mlp_pre[layer], g_mlp_post[layer],
                           g_ple_pre[layer], g_ple_post[layer]])
        x2d = _post_mix(x2d, y_sb.reshape(n, -1), y_ml.reshape(n, -1), gates,
                        p[layer].reshape(n, -1),
                        w_branch_sb[layer].astype(bf16), w_branch_ml[layer].astype(bf16),
                        w_out[layer].astype(bf16), w_mlp_up[layer].astype(bf16),
                        w_mlp_down[layer].astype(bf16), w_ple_gate[layer].astype(bf16),
                        w_ple_proj[layer].astype(bf16), gains)
    return x2d.reshape(bsz, seq, d)
```
